```python
import math
import jax
import jax.numpy as jnp
from jax import lax
import numpy as np

D_MODEL = 1024
BATCH = 8
SEQ = 4096
DEPTH = 2

GRID_W = 64
CTX_LEN = 256
HEAD_DIM = 64
D_HY = 3 * D_MODEL // 8
D_NA = 3 * D_MODEL // 8
D_GA = D_MODEL - D_HY - D_NA
N_NA = D_NA // HEAD_DIM
N_GA = D_GA // HEAD_DIM
KH_MAX = 8
KW = 16
HY_EMB = 33
HY_ORDER = 64
HY_INNER = 2
HY_FAST_DECAY = 0.3
HY_SLOW_DECAY = 1.5
HY_TARGET = 1e-2
SHORT_K = 3
D_FF = 2816
N_EXPERTS = 8
TOP_K = 2
D_FF_E = 3584
ROPE_BASE = 10000.0
Q_BLOCK = 128
N_MOD = 6
EPS = 1e-6

kernel_name = "hybrid_hyena_natten_global_moe_dit"


def rms_norm(x, g):
    xf = x.astype(jnp.float32)
    y = xf * lax.rsqrt(jnp.mean(xf * xf, axis=-1, keepdims=True) + EPS)
    return (y * g.astype(jnp.float32)).astype(x.dtype)


def modulate(x, g, shift, scale):
    return rms_norm(x, g) * (1.0 + scale) + shift


def split_proj(p):
    B, L, _ = p.shape
    hy = p[..., :3 * D_HY]
    na = p[..., 3 * D_HY:3 * (D_HY + D_NA)].reshape(B, L, 3, N_NA, HEAD_DIM)
    ga = p[..., 3 * (D_HY + D_NA):].reshape(B, L, 3, N_GA, HEAD_DIM)
    return hy, na[:, :, 0], na[:, :, 1], na[:, :, 2], ga[:, :, 0], ga[:, :, 1], ga[:, :, 2]


def short_conv(u, w, b):
    L = u.shape[1]
    pad = SHORT_K // 2
    up = jnp.pad(u, ((0, 0), (pad, SHORT_K - 1 - pad), (0, 0)))
    y = b
    for j in range(SHORT_K):
        y = y + up[:, j:j + L] * w[j]
    return y


def hyena_filter(L, w1, b1, freq, w2, b2, w3):
    f32 = jnp.float32
    bands = (HY_EMB - 1) // 2
    t = jnp.linspace(0.0, 1.0, L, dtype=f32)[:, None]
    ang = 2.0 * math.pi * jnp.arange(L, dtype=f32)[:, None] / L
    fr = jnp.linspace(1e-4, bands - 1, bands, dtype=f32)[None, :]
    z = jnp.concatenate([t, jnp.cos(fr * ang), -jnp.sin(fr * ang)], axis=-1)
    w = freq.astype(f32)
    h = jnp.sin(w * (z @ w1.astype(f32) + b1.astype(f32)))
    for i in range(HY_INNER):
        h = jnp.sin(w * (h @ w2[i].astype(f32) + b2[i].astype(f32)))
    h = h @ w3.astype(f32)
    deltas = jnp.linspace(math.log(HY_TARGET) / HY_SLOW_DECAY,
                          math.log(HY_TARGET) / HY_FAST_DECAY, D_HY, dtype=f32)
    deltas = jnp.concatenate([deltas, deltas])
    return h * jnp.exp(-t * jnp.abs(deltas))


def bidir_long_conv(u, h, skip):
    L, C = u.shape[1], u.shape[2]
    h_fwd, h_bwd = h[:, :C], h[:, C:]
    k_circ = jnp.concatenate([h_fwd, jnp.zeros((1, C), jnp.float32), h_bwd[1:][::-1]], axis=0)
    uf = jnp.fft.rfft(u.astype(jnp.float32), n=2 * L, axis=1)
    kf = jnp.fft.rfft(k_circ, n=2 * L, axis=0)
    y = jnp.fft.irfft(uf * kf[None], n=2 * L, axis=1)[:, :L]
    return (y + u.astype(jnp.float32) * skip.astype(jnp.float32)).astype(u.dtype)


def hyena_mixer(p, conv_w, conv_b, filt, skip):
    u = short_conv(p, conv_w, conv_b)
    x0, x1, v = jnp.split(u, 3, axis=-1)
    v = bidir_long_conv(v * x1, filt, skip)
    return v * x0


def context_attention(q, k, v):
    B, Lc, H, Dh = q.shape
    s = jnp.einsum('bqhd,bkhd->bhqk', q, k).astype(jnp.float32) * (Dh ** -0.5)
    p = jax.nn.softmax(s, axis=-1).astype(v.dtype)
    return jnp.einsum('bhqk,bkhd->bqhd', p, v).reshape(B, Lc, H * Dh)


def neighbourhood_attention(q, k, v, k_ctx, v_ctx, rpb):
    B, S, H, Dh = q.shape
    rows = S // GRID_W
    kh = min(KH_MAX, rows)
    r = jnp.arange(rows)
    cidx = jnp.arange(GRID_W)
    row_idx = jnp.clip(r - kh // 2, 0, rows - kh)[:, None] + jnp.arange(kh)[None, :]
    col_start = jnp.clip(cidx - KW // 2, 0, GRID_W - KW)
    col_mask = (cidx[None, :] >= col_start[:, None]) & (cidx[None, :] < col_start[:, None] + KW)
    qg = q.reshape(B, rows, GRID_W, H, Dh)
    kb = k.reshape(B, rows, GRID_W, H, Dh)[:, row_idx]
    vb = v.reshape(B, rows, GRID_W, H, Dh)[:, row_idx]
    scale = Dh ** -0.5
    s_lat = jnp.einsum('brqhd,brkwhd->bhrqkw', qg, kb).astype(jnp.float32) * scale
    dr = row_idx - r[:, None] + (KH_MAX - 1)
    dc = jnp.clip(cidx[None, :] - cidx[:, None], -(KW - 1), KW - 1) + (KW - 1)
    bias = rpb[:, dr[:, None, :, None], dc[None, :, None, :]]
    s_lat = jnp.where(col_mask[:, None, :], s_lat + bias.astype(jnp.float32), -jnp.inf)
    s_ctx = jnp.einsum('brqhd,bchd->bhrqc', qg, k_ctx).astype(jnp.float32) * scale
    n_lat = kh * GRID_W
    s = jnp.concatenate([s_lat.reshape(B, H, rows, GRID_W, n_lat), s_ctx], axis=-1)
    p = jax.nn.softmax(s, axis=-1).astype(v.dtype)
    p_lat = p[..., :n_lat].reshape(B, H, rows, GRID_W, kh, GRID_W)
    o = (jnp.einsum('bhrqkw,brkwhd->brqhd', p_lat, vb)
         + jnp.einsum('bhrqc,bchd->brqhd', p[..., n_lat:], v_ctx))
    return o.reshape(B, S, H * Dh)


def global_attention(q, k, v, k_ctx, v_ctx):
    B, S, H, Dh = q.shape
    nb = S // Q_BLOCK
    scale = Dh ** -0.5
    qb = q.reshape(B, nb, Q_BLOCK, H, Dh).transpose(1, 0, 2, 3, 4)

    def block(qi):
        s = jnp.concatenate([jnp.einsum('bqhd,bkhd->bhqk', qi, k),
                             jnp.einsum('bqhd,bchd->bhqc', qi, k_ctx)], axis=-1).astype(jnp.float32) * scale
        p = jax.nn.softmax(s, axis=-1).astype(v.dtype)
        return (jnp.einsum('bhqk,bkhd->bqhd', p[..., :S], v)
                + jnp.einsum('bhqc,bchd->bqhd', p[..., S:], v_ctx))

    o = lax.map(block, qb)
    return o.transpose(1, 0, 2, 3, 4).reshape(B, S, H * Dh)


def rope_1d(x, pos):
    n = x.shape[-1] // 2
    inv = jnp.power(ROPE_BASE, -jnp.arange(n, dtype=jnp.float32) / n)
    ang = pos[:, None] * inv[None, :]
    cos = jnp.cos(ang)[None, :, None, :]
    sin = jnp.sin(ang)[None, :, None, :]
    x1, x2 = x[..., :n], x[..., n:]
    return jnp.concatenate([x1 * cos - x2 * sin, x1 * sin + x2 * cos], axis=-1)


def axial_rope(x, pos_row, pos_col):
    xf = x.astype(jnp.float32)
    half = x.shape[-1] // 2
    return jnp.concatenate([rope_1d(xf[..., :half], pos_row),
                            rope_1d(xf[..., half:], pos_col)], axis=-1).astype(x.dtype)


def swiglu(h, w_gate, w_up, w_down):
    return (jax.nn.silu(h @ w_gate) * (h @ w_up)) @ w_down


def moe_ffn(h, w_router, w_gate, w_up, w_down):
    logits = (h @ w_router).astype(jnp.float32)
    top_v, top_i = lax.top_k(logits, TOP_K)
    top_w = jax.nn.softmax(top_v, axis=-1)
    combine = jnp.sum(jax.nn.one_hot(top_i, N_EXPERTS, dtype=jnp.float32) * top_w[..., None], axis=-2)
    combine = combine.astype(h.dtype)
    out = jnp.zeros_like(h)
    for e in range(N_EXPERTS):
        out = out + combine[..., e:e + 1] * swiglu(h, w_gate[e], w_up[e], w_down[e])
    return out


def setup_inputs(seed: int = 0) -> dict:
    key = jax.random.key(seed)
    ks = iter(jax.random.split(key, 40))
    f32 = jnp.float32

    def nrm(shape, scale):
        return jax.random.normal(next(ks), shape, f32) * scale

    def gain(shape):
        return 1.0 + nrm(shape, 0.02)

    L = DEPTH
    nd = (DEPTH + 1) // 2
    nm = DEPTH // 2
    return {
        "x": nrm((BATCH, SEQ, D_MODEL), 1.0),
        "c": nrm((BATCH, D_MODEL), 1.0),
        "ctx": nrm((BATCH, CTX_LEN, D_MODEL), 1.0),
        "c_ctx": nrm((D_MODEL,), 1.0),
        "w_mod": nrm((L, D_MODEL, N_MOD * D_MODEL), 0.3 * D_MODEL ** -0.5),
        "b_mod": nrm((L, N_MOD * D_MODEL), 0.02),
        "norm_mix": gain((L, D_MODEL)),
        "norm_ffn": gain((L, D_MODEL)),
        "w_in": nrm((L, D_MODEL, 3 * D_MODEL), D_MODEL ** -0.5),
        "hy_conv_w": nrm((L, SHORT_K, 3 * D_HY), SHORT_K ** -0.5),
        "hy_conv_b": nrm((L, 3 * D_HY), 0.02),
        "hy_w1": nrm((L, HY_EMB, HY_ORDER), HY_EMB ** -0.5),
        "hy_b1": nrm((L, HY_ORDER), 0.2),
        "hy_freq": gain((L, HY_ORDER)),
        "hy_w2": nrm((L, HY_INNER, HY_ORDER, HY_ORDER), HY_ORDER ** -0.5),
        "hy_b2": nrm((L, HY_INNER, HY_ORDER), 0.2),
        "hy_w3": nrm((L, HY_ORDER, 2 * D_HY), HY_ORDER ** -0.5),
        "hy_skip": nrm((L, D_HY), 1.0),
        "na_q_norm": gain((L, HEAD_DIM)),
        "na_k_norm": gain((L, HEAD_DIM)),
        "na_rpb": nrm((L, N_NA, 2 * KH_MAX - 1, 2 * KW - 1), 0.02),
        "ga_q_norm": gain((L, HEAD_DIM)),
        "ga_k_norm": gain((L, HEAD_DIM)),
        "out_norm_hy": gain((L, D_HY)),
        "out_norm_na": gain((L, D_NA)),
        "out_norm_ga": gain((L, D_GA)),
        "w_out": nrm((L, D_MODEL, D_MODEL), D_MODEL ** -0.5),
        "ffn_w_gate": nrm((nd, D_MODEL, D_FF), D_MODEL ** -0.5),
        "ffn_w_up": nrm((nd, D_MODEL, D_FF), D_MODEL ** -0.5),
        "ffn_w_down": nrm((nd, D_FF, D_MODEL), D_FF ** -0.5),
        "moe_router": nrm((nm, D_MODEL, N_EXPERTS), D_MODEL ** -0.5),
        "moe_w_gate": nrm((nm, N_EXPERTS, D_MODEL, D_FF_E), D_MODEL ** -0.5),
        "moe_w_up": nrm((nm, N_EXPERTS, D_MODEL, D_FF_E), D_MODEL ** -0.5),
        "moe_w_down": nrm((nm, N_EXPERTS, D_FF_E, D_MODEL), D_FF_E ** -0.5),
    }


def reference(x, c, ctx, c_ctx, w_mod, b_mod, norm_mix, norm_ffn, w_in,
              hy_conv_w, hy_conv_b, hy_w1, hy_b1, hy_freq, hy_w2, hy_b2, hy_w3, hy_skip,
              na_q_norm, na_k_norm, na_rpb, ga_q_norm, ga_k_norm,
              out_norm_hy, out_norm_na, out_norm_ga, w_out,
              ffn_w_gate, ffn_w_up, ffn_w_down,
              moe_router, moe_w_gate, moe_w_up, moe_w_down):
    S = x.shape[1]
    Lc = ctx.shape[1]
    t = jnp.arange(S)
    pos_row = (t // GRID_W).astype(jnp.float32)
    pos_col = (t % GRID_W).astype(jnp.float32)

    for l in range(DEPTH):
        last = l == DEPTH - 1
        mod_x = jnp.split((jax.nn.silu(c) @ w_mod[l] + b_mod[l])[:, None, :], N_MOD, axis=-1)
        mod_c = jnp.split((jax.nn.silu(c_ctx) @ w_mod[l] + b_mod[l])[None, None, :], N_MOD, axis=-1)

        hx = modulate(x, norm_mix[l], mod_x[0], mod_x[1])
        hc = modulate(ctx, norm_mix[l], mod_c[0], mod_c[1])
        hy_x, qn_x, kn_x, vn_x, qg_x, kg_x, vg_x = split_proj(hx @ w_in[l])
        hy_c, qn_c, kn_c, vn_c, qg_c, kg_c, vg_c = split_proj(hc @ w_in[l])
        qn_x, kn_x = rms_norm(qn_x, na_q_norm[l]), rms_norm(kn_x, na_k_norm[l])
        qn_c, kn_c = rms_norm(qn_c, na_q_norm[l]), rms_norm(kn_c, na_k_norm[l])
        qg_x, kg_x = rms_norm(qg_x, ga_q_norm[l]), rms_norm(kg_x, ga_k_norm[l])
        qg_c, kg_c = rms_norm(qg_c, ga_q_norm[l]), rms_norm(kg_c, ga_k_norm[l])
        qg_x = axial_rope(qg_x, pos_row, pos_col)
        kg_x = axial_rope(kg_x, pos_row, pos_col)

        filt_x = hyena_filter(S, hy_w1[l], hy_b1[l], hy_freq[l], hy_w2[l], hy_b2[l], hy_w3[l])
        y_hy = hyena_mixer(hy_x, hy_conv_w[l], hy_conv_b[l], filt_x, hy_skip[l])
        y_na = neighbourhood_attention(qn_x, kn_x, vn_x, kn_c, vn_c, na_rpb[l])
        y_ga = global_attention(qg_x, kg_x, vg_x, kg_c, vg_c)
        y = jnp.concatenate([rms_norm(y_hy, out_norm_hy[l]), rms_norm(y_na, out_norm_na[l]),
                             rms_norm(y_ga, out_norm_ga[l])], axis=-1) @ w_out[l]
        x = x + mod_x[2] * y

        if not last:
            filt_c = hyena_filter(Lc, hy_w1[l], hy_b1[l], hy_freq[l], hy_w2[l], hy_b2[l], hy_w3[l])
            yc_hy = hyena_mixer(hy_c, hy_conv_w[l], hy_conv_b[l], filt_c, hy_skip[l])
            yc_na = context_attention(qn_c, kn_c, vn_c)
            yc_ga = context_attention(qg_c, kg_c, vg_c)
            yc = jnp.concatenate([rms_norm(yc_hy, out_norm_hy[l]), rms_norm(yc_na, out_norm_na[l]),
                                  rms_norm(yc_ga, out_norm_ga[l])], axis=-1) @ w_out[l]
            ctx = ctx + mod_c[2] * yc

        if l % 2 == 0:
            i = l // 2
            ffn = functools_partial_dense(ffn_w_gate[i], ffn_w_up[i], ffn_w_down[i])
        else:
            i = l // 2
            ffn = functools_partial_moe(moe_router[i], moe_w_gate[i], moe_w_up[i], moe_w_down[i])
        x = x + mod_x[5] * ffn(modulate(x, norm_ffn[l], mod_x[3], mod_x[4]))
        if not last:
            ctx = ctx + mod_c[5] * ffn(modulate(ctx, norm_ffn[l], mod_c[3], mod_c[4]))

    return x


def functools_partial_dense(w_gate, w_up, w_down):
    def f(h):
        return swiglu(h, w_gate, w_up, w_down)
    return f


def functools_partial_moe(w_router, w_gate, w_up, w_down):
    def f(h):
        return moe_ffn(h, w_router, w_gate, w_up, w_down)
    return f
```

```python
import functools
import math

import jax
import jax.numpy as jnp
import numpy as np
from jax import lax
from jax.experimental import pallas as pl
from jax.experimental.pallas import tpu as pltpu

F32 = jnp.float32
BF16 = jnp.bfloat16

D_MODEL = 1024
GRID_W = 64
HEAD_DIM = 64
D_HY = 384
D_NA = 384
D_GA = 256
N_NA = 6
N_GA = 4
KH_MAX = 8
KW = 16
HY_EMB = 33
HY_FAST_DECAY = 0.3
HY_SLOW_DECAY = 1.5
HY_TARGET = 1e-2
N_EXPERTS = 8
ROPE_BASE = 10000.0
N_MOD = 6
EPS = 1e-6

LANES = 128
MXU_W = 256
D_PROJ = 3 * D_MODEL
C_NAQ = 3 * D_HY
C_NAK = C_NAQ + D_NA
C_NAV = C_NAK + D_NA
C_GAQ = C_NAV + D_NA
C_GAK = C_GAQ + D_GA
C_GAV = C_GAK + D_GA
NEG_INF = float("-inf")
VMEM_LIMIT = 56 * 1024 * 1024


def _cparams(n_axes):
    return pltpu.CompilerParams(dimension_semantics=("arbitrary",) * n_axes, vmem_limit_bytes=VMEM_LIMIT)


def _dot(a, b):
    return jnp.dot(a, b, preferred_element_type=F32)


def _dot_nt(a, b):
    return lax.dot_general(a, b, (((1,), (1,)), ((), ())), preferred_element_type=F32)


def _split(a):
    hi = a.astype(BF16)
    lo = (a - hi.astype(F32)).astype(BF16)
    return hi, lo


def _dot3(a, b):
    a_hi, a_lo = _split(a)
    b_hi, b_lo = _split(b)
    return _dot(a_hi, b_hi) + (_dot(a_hi, b_lo) + _dot(a_lo, b_hi))


def _mod_kernel(c_ref, w_ref, b_ref, o_ref):
    c = c_ref[...]
    s = c * jax.nn.sigmoid(c)
    o_ref[0] = _dot3(s, w_ref[0]) + b_ref[0]


def _mod_vectors(cvec, w_mod, b_mod):
    depth, d, n = w_mod.shape
    tn = 1024
    rows = cvec.shape[0]
    return pl.pallas_call(
        _mod_kernel,
        out_shape=jax.ShapeDtypeStruct((depth, rows, n), F32),
        grid=(depth, n // tn),
        in_specs=[pl.BlockSpec((rows, d), lambda l, j: (0, 0)),
                  pl.BlockSpec((1, d, tn), lambda l, j: (l, 0, j)),
                  pl.BlockSpec((1, 1, tn), lambda l, j: (l, 0, j))],
        out_specs=pl.BlockSpec((1, rows, tn), lambda l, j: (l, 0, j)),
        compiler_params=_cparams(2),
        name="mod_vectors",
    )(cvec, w_mod, b_mod.reshape(depth, 1, n))


def _rope_kernel(cos_ref, sin_ref):
    shape = cos_ref.shape
    t = lax.broadcasted_iota(jnp.int32, shape, 0)
    d = lax.broadcasted_iota(jnp.int32, shape, 1) & (HEAD_DIM - 1)
    pos = jnp.where(d < HEAD_DIM // 2, t // GRID_W, t % GRID_W).astype(F32)
    n = HEAD_DIM // 4
    inv = jnp.exp((d & (n - 1)).astype(F32) * (-math.log(ROPE_BASE) / n))
    ang = pos * inv
    s = jnp.sin(ang)
    cos_ref[...] = jnp.cos(ang)
    sin_ref[...] = jnp.where((d & (2 * n - 1)) < n, -s, s)


def _rope_tables(seq):
    return pl.pallas_call(
        _rope_kernel,
        out_shape=(jax.ShapeDtypeStruct((seq, LANES), F32), jax.ShapeDtypeStruct((seq, LANES), F32)),
        name="rope_tables",
    )()


def _in_kernel(x_ref, m_ref, g_ref, w_ref, gain_ref, *rest, rope):
    if rope:
        cos_ref, sin_ref, o_ref = rest
    else:
        (o_ref,) = rest
    xf = x_ref[0]
    ms = jnp.mean(xf * xf, axis=-1, keepdims=True)
    y = xf * lax.rsqrt(ms + EPS) * g_ref[...]
    h = (y * (1.0 + m_ref[0, 1:2, :]) + m_ref[0, 0:1, :]).astype(BF16)

    ri = lax.broadcasted_iota(jnp.int32, (MXU_W, MXU_W), 0) // HEAD_DIM
    ci = lax.broadcasted_iota(jnp.int32, (MXU_W, MXU_W), 1) // HEAD_DIM
    gmat = jnp.where(ri == ci, 1.0, 0.0).astype(BF16)

    def proj(a, b):
        return _dot(h, w_ref[:, a:b])

    def head_norm(c, a):
        outs = []
        for s in range(0, c.shape[1], MXU_W):
            cc = c[:, s:s + MXU_W]
            ss = _dot((cc * cc).astype(BF16), gmat)
            outs.append(cc * lax.rsqrt(ss * (1.0 / HEAD_DIM) + EPS))
        return jnp.concatenate(outs, axis=-1) * gain_ref[:, a:a + c.shape[1]]

    def rotate(c):
        lane = lax.broadcasted_iota(jnp.int32, (c.shape[0], LANES), 1)
        first = (lane & (HEAD_DIM // 2 - 1)) < HEAD_DIM // 4
        cos = cos_ref[...]
        sin = sin_ref[...]
        outs = []
        for s in range(0, c.shape[1], LANES):
            cc = c[:, s:s + LANES]
            partner = jnp.where(first, pltpu.roll(cc, LANES - HEAD_DIM // 4, 1), pltpu.roll(cc, HEAD_DIM // 4, 1))
            outs.append(cc * cos + partner * sin)
        return jnp.concatenate(outs, axis=-1)

    o_ref[0, :, 0:C_NAQ] = proj(0, C_NAQ).astype(BF16)
    o_ref[0, :, C_NAQ:C_NAV] = head_norm(proj(C_NAQ, C_NAV), C_NAQ).astype(BF16)
    o_ref[0, :, C_NAV:C_GAQ] = proj(C_NAV, C_GAQ).astype(BF16)
    qk = head_norm(proj(C_GAQ, C_GAV), C_GAQ)
    if rope:
        qk = rotate(qk)
    o_ref[0, :, C_GAQ:C_GAV] = qk.astype(BF16)
    o_ref[0, :, C_GAV:D_PROJ] = proj(C_GAV, D_PROJ).astype(BF16)


def _in_proj(x, mods, mod_row, g, w, gain, rope_tabs):
    bsz, seq, d = x.shape
    tm = min(512, seq)
    rope = rope_tabs is not None
    in_specs = [pl.BlockSpec((1, tm, d), lambda b, i: (b, i, 0)),
                pl.BlockSpec((1, N_MOD, d), lambda b, i: (mod_row(b), 0, 0)),
                pl.BlockSpec((1, d), lambda b, i: (0, 0)),
                pl.BlockSpec((d, D_PROJ), lambda b, i: (0, 0)),
                pl.BlockSpec((1, D_PROJ), lambda b, i: (0, 0))]
    args = [x, mods, g, w, gain]
    if rope:
        in_specs += [pl.BlockSpec((tm, LANES), lambda b, i: (i, 0))] * 2
        args += list(rope_tabs)
    return pl.pallas_call(
        functools.partial(_in_kernel, rope=rope),
        out_shape=jax.ShapeDtypeStruct((bsz, seq, D_PROJ), BF16),
        grid=(bsz, seq // tm),
        in_specs=in_specs,
        out_specs=pl.BlockSpec((1, tm, D_PROJ), lambda b, i: (b, i, 0)),
        compiler_params=_cparams(2),
        name="in_proj",
    )(*args)


def _hy_sizes(seq):
    bk = min(512, seq // 2)
    return bk, seq // bk, bk + 8


def _dft_kernel(fwd_ref, inv_ref, *, bk, nf):
    n2 = 2 * bk
    k = lax.broadcasted_iota(jnp.int32, (nf, n2), 0)
    n = lax.broadcasted_iota(jnp.int32, (nf, n2), 1)
    ang = ((k * n) & (n2 - 1)).astype(F32) * (2.0 * math.pi / n2)
    live = k <= bk
    fwd_ref[0:nf, :] = jnp.where(live, jnp.cos(ang), 0.0)
    fwd_ref[nf:2 * nf, :] = jnp.where(live, -jnp.sin(ang), 0.0)
    m = lax.broadcasted_iota(jnp.int32, (bk, nf), 0) + bk
    k = lax.broadcasted_iota(jnp.int32, (bk, nf), 1)
    ang = ((k * m) & (n2 - 1)).astype(F32) * (2.0 * math.pi / n2)
    wk = jnp.where((k == 0) | (k == bk), 1.0 / n2, jnp.where(k < bk, 2.0 / n2, 0.0))
    inv_ref[:, 0:nf] = wk * jnp.cos(ang)
    inv_ref[:, nf:2 * nf] = -wk * jnp.sin(ang)


def _dft_mats(seq):
    bk, _, nf = _hy_sizes(seq)
    return pl.pallas_call(
        functools.partial(_dft_kernel, bk=bk, nf=nf),
        out_shape=(jax.ShapeDtypeStruct((2 * nf, 2 * bk), F32), jax.ShapeDtypeStruct((bk, 2 * nf), F32)),
        compiler_params=_cparams(0),
        name="dft_mats",
    )()


def _filt_kernel(fr_ref, w1_ref, b1_ref, fq_ref, w2_ref, b2_ref, w3f_ref, w3b_ref, dl_ref, fwd_ref, g_ref,
                 *, seq, bk, nj):
    n2 = 2 * bk
    d = pl.program_id(1) - (nj - 1)
    r = lax.broadcasted_iota(jnp.int32, (n2, LANES), 0)
    lane = lax.broadcasted_iota(jnp.int32, (n2, LANES), 1)
    n = (d - 1) * bk + r
    an = jnp.abs(n).astype(F32)
    t = an / (seq - 1.0)
    ang = an * (2.0 * math.pi / seq)
    bands = (HY_EMB - 1) // 2
    arg = fr_ref[...] * ang
    z = jnp.where(lane == 0, t, jnp.where(lane <= bands, jnp.cos(arg), jnp.where(lane <= 2 * bands, -jnp.sin(arg), 0.0)))
    fq = fq_ref[...]
    h = jnp.sin(fq * (_dot3(z, w1_ref[...]) + b1_ref[...]))
    for i in range(w2_ref.shape[0]):
        h = jnp.sin(fq * (_dot3(h, w2_ref[i]) + b2_ref[i]))
    decay = jnp.exp(-t * jnp.abs(dl_ref[...]))
    hf = _dot3(h, w3f_ref[...]) * decay
    hb = _dot3(h, w3b_ref[...]) * decay
    g = jnp.where(n >= 0, hf, jnp.where(n > -seq, hb, 0.0))
    g_ref[0, 0] = _dot3(fwd_ref[...], g)


def _pad_to(a, shape):
    return jnp.pad(a, [(0, s - n) for n, s in zip(a.shape, shape)])


def _hyena_spectra(seq, fwd_full, w1, b1, freq, w2, b2, w3):
    bk, nj, nf = _hy_sizes(seq)
    nd = 2 * nj - 1
    bands = (HY_EMB - 1) // 2
    fr = jnp.linspace(1e-4, bands - 1, bands, dtype=F32)
    fr_row = _pad_to(jnp.concatenate([jnp.zeros((1,), F32), fr, fr]), (LANES,)).reshape(1, LANES)
    deltas = jnp.linspace(math.log(HY_TARGET) / HY_SLOW_DECAY, math.log(HY_TARGET) / HY_FAST_DECAY, D_HY,
                          dtype=F32).reshape(1, D_HY)
    ni = w2.shape[0]
    w1p = _pad_to(w1, (LANES, LANES))
    b1p = _pad_to(b1.reshape(1, -1), (1, LANES))
    fqp = _pad_to(freq.reshape(1, -1), (1, LANES))
    w2p = _pad_to(w2, (ni, LANES, LANES))
    b2p = _pad_to(b2.reshape(ni, 1, -1), (ni, 1, LANES))
    w3p = _pad_to(w3, (LANES, 2 * D_HY))
    nc = D_HY // LANES
    full = lambda shape: pl.BlockSpec(shape, lambda c, d: (0,) * len(shape))
    return pl.pallas_call(
        functools.partial(_filt_kernel, seq=seq, bk=bk, nj=nj),
        out_shape=jax.ShapeDtypeStruct((nc, nd, 2 * nf, LANES), F32),
        grid=(nc, nd),
        in_specs=[full((1, LANES)), full((LANES, LANES)), full((1, LANES)), full((1, LANES)),
                  full((ni, LANES, LANES)), full((ni, 1, LANES)),
                  pl.BlockSpec((LANES, LANES), lambda c, d: (0, c)),
                  pl.BlockSpec((LANES, LANES), lambda c, d: (0, nc + c)),
                  pl.BlockSpec((1, LANES), lambda c, d: (0, c)),
                  full((2 * nf, 2 * bk))],
        out_specs=pl.BlockSpec((1, 1, 2 * nf, LANES), lambda c, d: (c, d, 0, 0)),
        compiler_params=_cparams(2),
        name="hyena_spectra",
    )(fr_row, w1p, b1p, fqp, w2p, b2p, w3p, w3p, deltas, fwd_full)


def _hyena_kernel(p0_ref, p1_ref, p2_ref, w0_ref, w1_ref, w2_ref, b0_ref, b1_ref, b2_ref, skip_ref,
                  g_ref, fwd_ref, inv_ref, o_ref, z_s, x0_s, u_s, y_s, *, seq, bk, nj, nf):
    row = lax.broadcasted_iota(jnp.int32, (seq, LANES), 0)

    def short_conv(p_ref, w_ref, b_ref):
        p = p_ref[0].astype(F32)
        prev = jnp.where(row == 0, 0.0, pltpu.roll(p, 1, 0))
        nxt = jnp.where(row == seq - 1, 0.0, pltpu.roll(p, seq - 1, 0))
        return b_ref[...] + prev * w_ref[0:1, :] + p * w_ref[1:2, :] + nxt * w_ref[2:3, :]

    x0_s[...] = short_conv(p0_ref, w0_ref, b0_ref)
    z_s[...] = short_conv(p2_ref, w2_ref, b2_ref) * short_conv(p1_ref, w1_ref, b1_ref)

    fwd = fwd_ref[...]
    for j in range(nj):
        u_s[j] = _dot(fwd, z_s[j * bk:(j + 1) * bk, :].astype(BF16))

    inv = inv_ref[...]
    skip = skip_ref[...]
    rc = 40 if nf % 40 == 0 else nf
    for i in range(nj):
        def chunk(c, carry):
            r0 = pl.multiple_of(c * rc, 8)
            re = pl.ds(r0, rc)
            im = pl.ds(nf + r0, rc)
            acc_re = jnp.zeros((rc, LANES), F32)
            acc_im = jnp.zeros((rc, LANES), F32)
            for j in range(nj):
                d = i - j + nj - 1
                g_re = g_ref[0, d, re, :]
                g_im = g_ref[0, d, im, :]
                u_re = u_s[j, re, :]
                u_im = u_s[j, im, :]
                acc_re = acc_re + (g_re * u_re - g_im * u_im)
                acc_im = acc_im + (g_re * u_im + g_im * u_re)
            y_s[re, :] = acc_re
            y_s[im, :] = acc_im
            return carry

        lax.fori_loop(0, nf // rc, chunk, 0)
        rows = slice(i * bk, (i + 1) * bk)
        y = _dot(inv, y_s[...].astype(BF16))
        o_ref[0, rows, :] = ((y + z_s[rows, :] * skip) * x0_s[rows, :]).astype(BF16)


def _hyena(p, conv_w, conv_b, skip, spectra, fwd_full, inv_full):
    bsz, seq, _ = p.shape
    bk, nj, nf = _hy_sizes(seq)
    nd = 2 * nj - 1
    nc = D_HY // LANES
    fwd = fwd_full[:, :bk].astype(BF16)
    inv = inv_full.astype(BF16)
    k = conv_w.shape[0]
    part = lambda o: pl.BlockSpec((1, seq, LANES), lambda c, b: (b, 0, o * nc + c))
    wpart = lambda o: pl.BlockSpec((k, LANES), lambda c, b: (0, o * nc + c))
    bpart = lambda o: pl.BlockSpec((1, LANES), lambda c, b: (0, o * nc + c))
    conv_b = conv_b.reshape(1, -1)
    return pl.pallas_call(
        functools.partial(_hyena_kernel, seq=seq, bk=bk, nj=nj, nf=nf),
        out_shape=jax.ShapeDtypeStruct((bsz, seq, D_HY), BF16),
        grid=(nc, bsz),
        in_specs=[part(0), part(1), part(2), wpart(0), wpart(1), wpart(2), bpart(0), bpart(1), bpart(2),
                  pl.BlockSpec((1, LANES), lambda c, b: (0, c)),
                  pl.BlockSpec((1, nd, 2 * nf, LANES), lambda c, b: (c, 0, 0, 0)),
                  pl.BlockSpec((2 * nf, bk), lambda c, b: (0, 0)),
                  pl.BlockSpec((bk, 2 * nf), lambda c, b: (0, 0))],
        out_specs=pl.BlockSpec((1, seq, LANES), lambda c, b: (b, 0, c)),
        scratch_shapes=[pltpu.VMEM((seq, LANES), F32), pltpu.VMEM((seq, LANES), F32),
                        pltpu.VMEM((nj, 2 * nf, LANES), F32), pltpu.VMEM((2 * nf, LANES), F32)],
        compiler_params=_cparams(2),
        name="hyena",
    )(p, p, p, conv_w, conv_w, conv_w, conv_b, conv_b, conv_b, skip.reshape(1, -1), spectra, fwd, inv)


def _softmax_heads(q_ref, kv, bias_ref, o_ref):
    outs = []
    for hh in range(LANES // HEAD_DIM):
        cols = slice(hh * HEAD_DIM, (hh + 1) * HEAD_DIM)
        q = q_ref[0, :, cols]
        scores = [_dot_nt(q, k[:, cols]) for k, _ in kv]
        if bias_ref is not None:
            scores[0] = scores[0] + bias_ref[0, hh]
        m = functools.reduce(jnp.maximum, [jnp.max(s, axis=-1, keepdims=True) for s in scores])
        probs = [jnp.exp(s - m) for s in scores]
        denom = functools.reduce(jnp.add, [jnp.sum(e, axis=-1, keepdims=True) for e in probs])
        o = functools.reduce(jnp.add, [_dot(e.astype(BF16), v[:, cols]) for e, (_, v) in zip(probs, kv)])
        outs.append(o / denom)
    o_ref[0] = jnp.concatenate(outs, axis=-1).astype(BF16)


def _attn_kernel(q_ref, *rest):
    *kv_refs, o_ref = rest
    kv = [(kv_refs[i][0], kv_refs[i + 1][0]) for i in range(0, len(kv_refs), 2)]
    _softmax_heads(q_ref, kv, None, o_ref)


def _attention(q_src, q_col, kv_srcs, n_heads):
    bsz, lq, _ = q_src.shape
    tq = min(256, lq)
    npair = n_heads * HEAD_DIM // LANES
    qb = q_col // LANES
    in_specs = [pl.BlockSpec((1, tq, LANES), lambda b, h, i: (b, i, qb + h))]
    args = [q_src]
    for arr, k_col, v_col in kv_srcs:
        lk = arr.shape[1]
        for col in (k_col, v_col):
            cb = col // LANES
            in_specs.append(pl.BlockSpec((1, lk, LANES), lambda b, h, i, cb=cb: (b, 0, cb + h)))
            args.append(arr)
    return pl.pallas_call(
        _attn_kernel,
        out_shape=jax.ShapeDtypeStruct((bsz, lq, n_heads * HEAD_DIM), BF16),
        grid=(bsz, npair, lq // tq),
        in_specs=in_specs,
        out_specs=pl.BlockSpec((1, tq, LANES), lambda b, h, i: (b, i, h)),
        compiler_params=_cparams(3),
        name="attention",
    )(*args)


NA_QROWS = 8


def _na_plan(rows):
    kh = min(KH_MAX, rows)
    nkr = min(2 * NA_QROWS, rows)
    classes, cls, ks_list = [], [], []
    for r0 in range(0, rows, NA_QROWS):
        ks = int(np.clip(r0 - kh // 2, 0, rows - nkr))
        tab = []
        for a in range(NA_QROWS):
            r = r0 + a
            ws = int(np.clip(r - kh // 2, 0, rows - kh))
            tab.append([(ks + j) - r + (KH_MAX - 1) if ws <= ks + j < ws + kh else -1 for j in range(nkr)])
        tab = tuple(map(tuple, tab))
        if tab not in classes:
            classes.append(tab)
        cls.append(classes.index(tab))
        ks_list.append(ks)
    return np.array(classes, np.int32), np.array(cls, np.int32), np.array(ks_list, np.int32), nkr


def _na_bias_kernel(dr_ref, rpb_ref, o_ref, tab_s, *, nkr):
    c = pl.program_id(0)
    h = pl.program_id(1)
    ndr = 2 * KH_MAX - 1
    ndc = 2 * KW - 1
    cq = lax.broadcasted_iota(jnp.int32, (GRID_W, LANES), 0)
    lane = lax.broadcasted_iota(jnp.int32, (GRID_W, LANES), 1)
    ck = lane & (GRID_W - 1)
    dc = jnp.clip(ck - cq, -(KW - 1), KW - 1) + (KW - 1)
    start = jnp.clip(cq - KW // 2, 0, GRID_W - KW)
    for dr in range(ndr):
        acc = jnp.zeros((GRID_W, LANES), F32)
        for j in range(ndc):
            acc = jnp.where(dc == j, rpb_ref[(h * ndr + dr) * ndc + j], acc)
        acc = jnp.where(ck >= start, acc, NEG_INF)
        tab_s[dr] = jnp.where(ck < start + KW, acc, NEG_INF)
    left = lane < GRID_W
    for a in range(NA_QROWS):
        for jp in range(nkr // 2):
            base = (c * NA_QROWS + a) * nkr + 2 * jp
            d0 = dr_ref[base]
            d1 = dr_ref[base + 1]
            b0 = jnp.where(d0 >= 0, tab_s[jnp.maximum(d0, 0)], NEG_INF)
            b1 = jnp.where(d1 >= 0, tab_s[jnp.maximum(d1, 0)], NEG_INF)
            o_ref[0, 0, a * GRID_W:(a + 1) * GRID_W, jp * LANES:(jp + 1) * LANES] = jnp.where(left, b0, b1)


def _na_bias(rpb, classes, nkr):
    ncls = classes.shape[0]
    return pl.pallas_call(
        functools.partial(_na_bias_kernel, nkr=nkr),
        out_shape=jax.ShapeDtypeStruct((ncls, N_NA, NA_QROWS * GRID_W, nkr * GRID_W), F32),
        grid_spec=pltpu.PrefetchScalarGridSpec(
            num_scalar_prefetch=1,
            grid=(ncls, N_NA),
            in_specs=[pl.BlockSpec(memory_space=pltpu.SMEM)],
            out_specs=pl.BlockSpec((1, 1, NA_QROWS * GRID_W, nkr * GRID_W), lambda c, h, dr: (c, h, 0, 0)),
            scratch_shapes=[pltpu.VMEM((2 * KH_MAX - 1, GRID_W, LANES), F32)]),
        compiler_params=_cparams(2),
        name="na_bias",
    )(jnp.asarray(classes.reshape(-1)), rpb.reshape(-1))


def _na_kernel(cls_ref, ks_ref, q_ref, k_ref, v_ref, kc_ref, vc_ref, bias_ref, o_ref, *, nkeys):
    k0 = pl.multiple_of(ks_ref[pl.program_id(2)] * GRID_W, GRID_W)
    win = pl.ds(k0, nkeys)
    kv = [(k_ref[0, win, :], v_ref[0, win, :]), (kc_ref[0], vc_ref[0])]
    _softmax_heads(q_ref, kv, bias_ref, o_ref)


def _na_attention(p, p_ctx, bias, cls, ks, nkr):
    bsz, seq, _ = p.shape
    lc = p_ctx.shape[1]
    tq = NA_QROWS * GRID_W
    nkeys = nkr * GRID_W
    npair = D_NA // LANES
    qb, kb, vb = C_NAQ // LANES, C_NAK // LANES, C_NAV // LANES
    return pl.pallas_call(
        functools.partial(_na_kernel, nkeys=nkeys),
        out_shape=jax.ShapeDtypeStruct((bsz, seq, D_NA), BF16),
        grid_spec=pltpu.PrefetchScalarGridSpec(
            num_scalar_prefetch=2,
            grid=(bsz, npair, seq // tq),
            in_specs=[pl.BlockSpec((1, tq, LANES), lambda b, h, i, c, s: (b, i, qb + h)),
                      pl.BlockSpec((1, seq, LANES), lambda b, h, i, c, s: (b, 0, kb + h)),
                      pl.BlockSpec((1, seq, LANES), lambda b, h, i, c, s: (b, 0, vb + h)),
                      pl.BlockSpec((1, lc, LANES), lambda b, h, i, c, s: (b, 0, kb + h)),
                      pl.BlockSpec((1, lc, LANES), lambda b, h, i, c, s: (b, 0, vb + h)),
                      pl.BlockSpec((1, LANES // HEAD_DIM, tq, nkeys), lambda b, h, i, c, s: (c[i], h, 0, 0))],
            out_specs=pl.BlockSpec((1, tq, LANES), lambda b, h, i, c, s: (b, i, h))),
        compiler_params=_cparams(3),
        name="na_attention",
    )(jnp.asarray(cls), jnp.asarray(ks), p, p, p, p_ctx, p_ctx, bias)


def _out_kernel(hy_ref, na_ref, ga_ref, x_ref, m_ref, ghy_ref, gna_ref, gga_ref, w_ref, o_ref):
    def norm(y_ref, g_ref):
        y = y_ref[0].astype(F32)
        ms = jnp.mean(y * y, axis=-1, keepdims=True)
        return (y * lax.rsqrt(ms + EPS) * g_ref[...]).astype(BF16)

    yn = jnp.concatenate([norm(hy_ref, ghy_ref), norm(na_ref, gna_ref), norm(ga_ref, gga_ref)], axis=-1)
    o_ref[0] = x_ref[0] + m_ref[0, 2:3, :] * _dot(yn, w_ref[...])


def _out_proj(y_hy, y_na, y_ga, x, mods, mod_row, g_hy, g_na, g_ga, w):
    bsz, seq, d = x.shape
    tm = min(512, seq)
    tile = lambda n: pl.BlockSpec((1, tm, n), lambda b, i: (b, i, 0))
    vec = lambda n: pl.BlockSpec((1, n), lambda b, i: (0, 0))
    return pl.pallas_call(
        _out_kernel,
        out_shape=jax.ShapeDtypeStruct((bsz, seq, d), F32),
        grid=(bsz, seq // tm),
        in_specs=[tile(D_HY), tile(D_NA), tile(D_GA), tile(d),
                  pl.BlockSpec((1, N_MOD, d), lambda b, i: (mod_row(b), 0, 0)),
                  vec(D_HY), vec(D_NA), vec(D_GA),
                  pl.BlockSpec((d, d), lambda b, i: (0, 0))],
        out_specs=tile(d),
        compiler_params=_cparams(2),
        name="out_proj",
    )(y_hy, y_na, y_ga, x, mods, g_hy.reshape(1, -1), g_na.reshape(1, -1), g_ga.reshape(1, -1), w)


def _modulated(x_ref, m_ref, g_ref):
    xf = x_ref[0]
    ms = jnp.mean(xf * xf, axis=-1, keepdims=True)
    return xf * lax.rsqrt(ms + EPS) * g_ref[...] * (1.0 + m_ref[0, 4:5, :]) + m_ref[0, 3:4, :]


def _swiglu_tile(h, wg, wu, wd):
    a = _dot(h, wg)
    t = (a * jax.nn.sigmoid(a)) * _dot(h, wu)
    return _dot(t.astype(BF16), wd)


def _ffn_kernel(x_ref, m_ref, g_ref, wg_ref, wu_ref, wd_ref, o_ref, h_s, acc_s):
    k = pl.program_id(2)

    @pl.when(k == 0)
    def _():
        h_s[...] = _modulated(x_ref, m_ref, g_ref).astype(BF16)
        acc_s[...] = jnp.zeros_like(acc_s)

    acc_s[...] += _swiglu_tile(h_s[...], wg_ref[...], wu_ref[...], wd_ref[...])

    @pl.when(k == pl.num_programs(2) - 1)
    def _():
        o_ref[0] = x_ref[0] + m_ref[0, 5:6, :] * acc_s[...]


def _ffn(x, mods, mod_row, g, wg, wu, wd):
    bsz, seq, d = x.shape
    dff = wg.shape[1]
    tm = min(1024, seq)
    tf = 256
    return pl.pallas_call(
        _ffn_kernel,
        out_shape=jax.ShapeDtypeStruct((bsz, seq, d), F32),
        grid=(bsz, seq // tm, dff // tf),
        in_specs=[pl.BlockSpec((1, tm, d), lambda b, i, k: (b, i, 0)),
                  pl.BlockSpec((1, N_MOD, d), lambda b, i, k: (mod_row(b), 0, 0)),
                  pl.BlockSpec((1, d), lambda b, i, k: (0, 0)),
                  pl.BlockSpec((d, tf), lambda b, i, k: (0, k)),
                  pl.BlockSpec((d, tf), lambda b, i, k: (0, k)),
                  pl.BlockSpec((tf, d), lambda b, i, k: (k, 0))],
        out_specs=pl.BlockSpec((1, tm, d), lambda b, i, k: (b, i, 0)),
        scratch_shapes=[pltpu.VMEM((tm, d), BF16), pltpu.VMEM((tm, d), F32)],
        compiler_params=_cparams(3),
        name="ffn",
    )(x, mods, g, wg, wu, wd)


def _route_kernel(x_ref, m_ref, g_ref, wr_ref, h_ref, comb_ref):
    h = _modulated(x_ref, m_ref, g_ref)
    h_ref[0] = h.astype(BF16)
    lane = lax.broadcasted_iota(jnp.int32, (h.shape[0], LANES), 1)
    logits = jnp.where(lane < N_EXPERTS, _dot3(h, wr_ref[...]), NEG_INF)
    m1 = jnp.max(logits, axis=-1, keepdims=True)
    i1 = jnp.min(jnp.where(logits == m1, lane, LANES), axis=-1, keepdims=True)
    rest = jnp.where(lane == i1, NEG_INF, logits)
    m2 = jnp.max(rest, axis=-1, keepdims=True)
    i2 = jnp.min(jnp.where(rest == m2, lane, LANES), axis=-1, keepdims=True)
    e2 = jnp.exp(m2 - m1)
    w1 = 1.0 / (1.0 + e2)
    comb_ref[0] = jnp.where(lane == i1, w1, jnp.where(lane == i2, e2 * w1, 0.0))


def _route(x, mods, g, w_router):
    bsz, seq, d = x.shape
    tm = min(512, seq)
    wr = _pad_to(w_router, (d, LANES))
    return pl.pallas_call(
        _route_kernel,
        out_shape=(jax.ShapeDtypeStruct((bsz, seq, d), BF16), jax.ShapeDtypeStruct((bsz, seq, LANES), F32)),
        grid=(bsz, seq // tm),
        in_specs=[pl.BlockSpec((1, tm, d), lambda b, i: (b, i, 0)),
                  pl.BlockSpec((1, N_MOD, d), lambda b, i: (b, 0, 0)),
                  pl.BlockSpec((1, d), lambda b, i: (0, 0)),
                  pl.BlockSpec((d, LANES), lambda b, i: (0, 0))],
        out_specs=(pl.BlockSpec((1, tm, d), lambda b, i: (b, i, 0)),
                   pl.BlockSpec((1, tm, LANES), lambda b, i: (b, i, 0))),
        compiler_params=_cparams(2),
        name="moe_route",
    )(x, mods, g, wr)


def _moe_kernel(x_ref, h_ref, comb_ref, m_ref, wg_ref, wu_ref, wd_ref, o_ref, acc_s):
    e = pl.program_id(2)
    k = pl.program_id(3)

    @pl.when((e == 0) & (k == 0))
    def _():
        acc_s[...] = jnp.zeros_like(acc_s)

    comb = comb_ref[0]
    lane = lax.broadcasted_iota(jnp.int32, comb.shape, 1)
    cw = jnp.sum(jnp.where(lane == e, comb, 0.0), axis=-1, keepdims=True)
    acc_s[...] += cw * _swiglu_tile(h_ref[0], wg_ref[0], wu_ref[0], wd_ref[0])

    @pl.when((e == pl.num_programs(2) - 1) & (k == pl.num_programs(3) - 1))
    def _():
        o_ref[0] = x_ref[0] + m_ref[0, 5:6, :] * acc_s[...]


def _moe(x, h, comb, mods, wg, wu, wd):
    bsz, seq, d = x.shape
    ne, _, dff = wg.shape
    tm = min(1024, seq)
    tf = 512
    return pl.pallas_call(
        _moe_kernel,
        out_shape=jax.ShapeDtypeStruct((bsz, seq, d), F32),
        grid=(bsz, seq // tm, ne, dff // tf),
        in_specs=[pl.BlockSpec((1, tm, d), lambda b, i, e, k: (b, i, 0)),
                  pl.BlockSpec((1, tm, d), lambda b, i, e, k: (b, i, 0)),
                  pl.BlockSpec((1, tm, LANES), lambda b, i, e, k: (b, i, 0)),
                  pl.BlockSpec((1, N_MOD, d), lambda b, i, e, k: (b, 0, 0)),
                  pl.BlockSpec((1, d, tf), lambda b, i, e, k: (e, 0, k)),
                  pl.BlockSpec((1, d, tf), lambda b, i, e, k: (e, 0, k)),
                  pl.BlockSpec((1, tf, d), lambda b, i, e, k: (e, k, 0))],
        out_specs=pl.BlockSpec((1, tm, d), lambda b, i, e, k: (b, i, 0)),
        scratch_shapes=[pltpu.VMEM((tm, d), F32)],
        compiler_params=_cparams(4),
        name="moe_experts",
    )(x, h, comb, mods, wg, wu, wd)


def _proj_gain(na_q, na_k, ga_q, ga_k):
    scale = HEAD_DIM ** -0.5
    one = lambda n: jnp.ones((n,), F32)
    return jnp.concatenate([one(C_NAQ), jnp.tile(na_q * scale, N_NA), jnp.tile(na_k, N_NA), one(D_NA),
                            jnp.tile(ga_q * scale, N_GA), jnp.tile(ga_k, N_GA), one(D_GA)]).reshape(1, D_PROJ)


def kernel(x, c, ctx, c_ctx, w_mod, b_mod, norm_mix, norm_ffn, w_in, hy_conv_w, hy_conv_b, hy_w1, hy_b1, hy_freq, hy_w2, hy_b2, hy_w3, hy_skip, na_q_norm, na_k_norm, na_rpb, ga_q_norm, ga_k_norm, out_norm_hy, out_norm_na, out_norm_ga, w_out, ffn_w_gate, ffn_w_up, ffn_w_down, moe_router, moe_w_gate, moe_w_up, moe_w_down):
    bsz, seq, d = x.shape
    lc = ctx.shape[1]
    depth = w_mod.shape[0]
    rows = seq // GRID_W

    mod_rows = -(-(bsz + 1) // 8) * 8
    cvec = _pad_to(jnp.concatenate([c, c_ctx[None, :]], axis=0), (mod_rows, d))
    mods = _mod_vectors(cvec, w_mod, b_mod).reshape(depth, mod_rows, N_MOD, d)
    x_row = lambda b: b
    ctx_row = lambda b: bsz

    rope_tabs = _rope_tables(seq)
    fwd_x, inv_x = _dft_mats(seq)
    classes, cls, ks, nkr = _na_plan(rows)

    for l in range(depth):
        last = l == depth - 1
        m_l = mods[l]
        w_in_l = w_in[l].astype(BF16)
        w_out_l = w_out[l].astype(BF16)
        gain = _proj_gain(na_q_norm[l], na_k_norm[l], ga_q_norm[l], ga_k_norm[l])
        g_mix = norm_mix[l].reshape(1, d)
        g_ffn = norm_ffn[l].reshape(1, d)
        filt = (hy_w1[l], hy_b1[l], hy_freq[l], hy_w2[l], hy_b2[l], hy_w3[l])

        p_x = _in_proj(x, m_l, x_row, g_mix, w_in_l, gain, rope_tabs)
        p_c = _in_proj(ctx, m_l, ctx_row, g_mix, w_in_l, gain, None)

        y_hy = _hyena(p_x, hy_conv_w[l], hy_conv_b[l], hy_skip[l],
                      _hyena_spectra(seq, fwd_x, *filt), fwd_x, inv_x)
        y_na = _na_attention(p_x, p_c, _na_bias(na_rpb[l], classes, nkr), cls, ks, nkr)
        y_ga = _attention(p_x, C_GAQ, [(p_x, C_GAK, C_GAV), (p_c, C_GAK, C_GAV)], N_GA)
        x = _out_proj(y_hy, y_na, y_ga, x, m_l, x_row, out_norm_hy[l], out_norm_na[l], out_norm_ga[l], w_out_l)

        if not last:
            fwd_c, inv_c = _dft_mats(lc)
            yc_hy = _hyena(p_c, hy_conv_w[l], hy_conv_b[l], hy_skip[l],
                           _hyena_spectra(lc, fwd_c, *filt), fwd_c, inv_c)
            yc_na = _attention(p_c, C_NAQ, [(p_c, C_NAK, C_NAV)], N_NA)
            yc_ga = _attention(p_c, C_GAQ, [(p_c, C_GAK, C_GAV)], N_GA)
            ctx = _out_proj(yc_hy, yc_na, yc_ga, ctx, m_l, ctx_row,
                            out_norm_hy[l], out_norm_na[l], out_norm_ga[l], w_out_l)

        i = l // 2
        if l % 2 == 0:
            wg, wu, wd = ffn_w_gate[i].astype(BF16), ffn_w_up[i].astype(BF16), ffn_w_down[i].astype(BF16)
            x = _ffn(x, m_l, x_row, g_ffn, wg, wu, wd)
            if not last:
                ctx = _ffn(ctx, m_l, ctx_row, g_ffn, wg, wu, wd)
        else:
            wg, wu, wd = moe_w_gate[i].astype(BF16), moe_w_up[i].astype(BF16), moe_w_down[i].astype(BF16)
            h, comb = _route(x, m_l, g_ffn, moe_router[i])
            x = _moe(x, h, comb, m_l, wg, wu, wd)
            if not last:
                hc, comb_c = _route(ctx, jnp.broadcast_to(m_l[bsz], (bsz, N_MOD, d)), g_ffn, moe_router[i])
                ctx = _moe(ctx, hc, comb_c, jnp.broadcast_to(m_l[bsz], (bsz, N_MOD, d)), wg, wu, wd)

    return x
```
